```python
import jax, jax.numpy as jnp
from jax import lax
import numpy as np

D_MODEL = 1024
BATCH = 8
SEQ = 4096
DEPTH = 4

N_MIXERS = 2
N_ATTN_LAYERS = (DEPTH + 1) // 2
N_SGU_LAYERS = DEPTH // 2

N_HEADS = 16
N_KV_HEADS = 4
HEAD_DIM = 64
Q_PER_KV = N_HEADS // N_KV_HEADS
WINDOW = 128
BLOCK = 128
Q_DIM = N_HEADS * HEAD_DIM
KV_DIM = N_KV_HEADS * HEAD_DIM
QKV_DIM = Q_DIM + 2 * KV_DIM
ALIBI_MAX_BIAS = 8.0

CHUNK = 128
SGU_HALF = 3 * D_MODEL
N_SGU_GROUPS = 8
SGU_GROUP_DIM = SGU_HALF // N_SGU_GROUPS

D_FF = 2816
CONV_WIDTH = 3

EPS = 1e-6

kernel_name = "hybrid_swa_sgu_convffn_trunk"


def rms_norm(x, g):
    xf = x.astype(jnp.float32)
    y = xf * lax.rsqrt(jnp.mean(xf * xf, axis=-1, keepdims=True) + EPS)
    return (y * g.astype(jnp.float32)).astype(x.dtype)


def layer_norm(x, g, b):
    xf = x.astype(jnp.float32)
    mu = jnp.mean(xf, axis=-1, keepdims=True)
    xc = xf - mu
    y = xc * lax.rsqrt(jnp.mean(xc * xc, axis=-1, keepdims=True) + EPS)
    return (y * g.astype(jnp.float32) + b.astype(jnp.float32)).astype(x.dtype)


def alibi_slopes():
    h = jnp.arange(1, N_HEADS + 1, dtype=jnp.float32)
    return jnp.exp2(-ALIBI_MAX_BIAS * h / N_HEADS)


def sliding_window_attention(h, w_qkv, b_qkv, sinks, w_o, b_o):
    B, S, _ = h.shape
    nb = S // BLOCK
    qkv = h @ w_qkv + b_qkv
    q, k, v = jnp.split(qkv, [Q_DIM, Q_DIM + KV_DIM], axis=-1)
    q = q.reshape(B, nb, BLOCK, N_KV_HEADS, Q_PER_KV, HEAD_DIM)
    k = k.reshape(B, nb, BLOCK, N_KV_HEADS, HEAD_DIM)
    v = v.reshape(B, nb, BLOCK, N_KV_HEADS, HEAD_DIM)

    def with_prev(t):
        prev = jnp.concatenate([jnp.zeros_like(t[:, :1]), t[:, :-1]], axis=1)
        return jnp.concatenate([prev, t], axis=2)

    kb, vb = with_prev(k), with_prev(v)
    scores = jnp.einsum("bnqgrd,bnkgd->bngrqk", q, kb).astype(jnp.float32) * (HEAD_DIM ** -0.5)

    qi = jnp.arange(BLOCK)[:, None]
    kj = jnp.arange(2 * BLOCK)[None, :]
    dist = qi + BLOCK - kj
    key_pos = jnp.arange(nb)[:, None, None] * BLOCK - BLOCK + kj[None]
    valid = (dist >= 0)[None] & (dist < WINDOW)[None] & (key_pos >= 0)

    slopes = alibi_slopes().reshape(N_KV_HEADS, Q_PER_KV)
    scores = scores - slopes[:, :, None, None] * dist.astype(jnp.float32)
    scores = jnp.where(valid[None, :, None, None], scores, -jnp.inf)

    sink = sinks.astype(jnp.float32).reshape(N_KV_HEADS, Q_PER_KV)[None, None, :, :, None, None]
    m = jnp.maximum(jnp.max(scores, axis=-1, keepdims=True), sink)
    p = jnp.exp(scores - m)
    probs = p / (jnp.sum(p, axis=-1, keepdims=True) + jnp.exp(sink - m))

    out = jnp.einsum("bngrqk,bnkgd->bnqgrd", probs.astype(h.dtype), vb)
    return out.reshape(B, S, Q_DIM) @ w_o + b_o


def chunked_sgu(h, w_in, ln_g, ln_b, w_s, b_s, w_out):
    B, S, _ = h.shape
    nc = S // CHUNK
    z = jax.nn.gelu(h @ w_in)
    u, v = jnp.split(z, 2, axis=-1)
    v = layer_norm(v, ln_g, ln_b)
    v = v.reshape(B, nc, CHUNK, N_SGU_GROUPS, SGU_GROUP_DIM)
    causal = jnp.tril(jnp.ones((CHUNK, CHUNK), dtype=bool))
    ws = jnp.where(causal[None], w_s, jnp.zeros_like(w_s))
    sv = jnp.einsum("gts,bcsgd->bctgd", ws, v) + b_s.T[None, None, :, :, None]
    return (u * sv.reshape(B, S, SGU_HALF)) @ w_out


def conv_gated_ffn(h, w_in, conv_w, conv_b, w_out):
    S = h.shape[1]
    g, u = jnp.split(h @ w_in, 2, axis=-1)
    gp = jnp.pad(g, ((0, 0), (CONV_WIDTH - 1, 0), (0, 0)))
    g = conv_b + sum(conv_w[k] * gp[:, k:k + S] for k in range(CONV_WIDTH))
    return (jax.nn.gelu(g) * u) @ w_out


def setup_inputs(seed: int = 0) -> dict:
    key = jax.random.key(seed)
    ks = jax.random.split(key, 24)
    f32 = jnp.float32

    def nrm(k, shape, scale):
        return jax.random.normal(k, shape, f32) * scale

    nA, nB = N_ATTN_LAYERS, N_SGU_LAYERS
    return {
        "x": nrm(ks[0], (BATCH, SEQ, D_MODEL), 1.0),
        "attn_w_qkv": nrm(ks[1], (nA, D_MODEL, QKV_DIM), D_MODEL ** -0.5),
        "attn_b_qkv": nrm(ks[2], (nA, QKV_DIM), 0.02),
        "attn_sinks": nrm(ks[3], (nA, N_HEADS), 1.0),
        "attn_w_o": nrm(ks[4], (nA, Q_DIM, D_MODEL), Q_DIM ** -0.5),
        "attn_b_o": nrm(ks[5], (nA, D_MODEL), 0.02),
        "sgu_w_in": nrm(ks[6], (nB, D_MODEL, 2 * SGU_HALF), D_MODEL ** -0.5),
        "sgu_ln_g": 1.0 + nrm(ks[7], (nB, SGU_HALF), 0.05),
        "sgu_ln_b": nrm(ks[8], (nB, SGU_HALF), 0.02),
        "sgu_w_s": nrm(ks[9], (nB, N_SGU_GROUPS, CHUNK, CHUNK), CHUNK ** -0.5),
        "sgu_b_s": 1.0 + nrm(ks[10], (nB, N_SGU_GROUPS, CHUNK), 0.05),
        "sgu_w_out": nrm(ks[11], (nB, SGU_HALF, D_MODEL), SGU_HALF ** -0.5),
        "ffn_w_in": nrm(ks[12], (DEPTH, D_MODEL, 2 * D_FF), D_MODEL ** -0.5),
        "ffn_conv_w": nrm(ks[13], (DEPTH, CONV_WIDTH, D_FF), CONV_WIDTH ** -0.5),
        "ffn_conv_b": nrm(ks[14], (DEPTH, D_FF), 0.02),
        "ffn_w_out": nrm(ks[15], (DEPTH, D_FF, D_MODEL), D_FF ** -0.5),
        "norm_mix_pre": 1.0 + nrm(ks[16], (DEPTH, D_MODEL), 0.05),
        "norm_mix_post": 1.0 + nrm(ks[17], (DEPTH, D_MODEL), 0.05),
        "norm_ffn_pre": 1.0 + nrm(ks[18], (DEPTH, D_MODEL), 0.05),
        "norm_ffn_post": 1.0 + nrm(ks[19], (DEPTH, D_MODEL), 0.05),
    }


def reference(x, attn_w_qkv, attn_b_qkv, attn_sinks, attn_w_o, attn_b_o,
              sgu_w_in, sgu_ln_g, sgu_ln_b, sgu_w_s, sgu_b_s, sgu_w_out,
              ffn_w_in, ffn_conv_w, ffn_conv_b, ffn_w_out,
              norm_mix_pre, norm_mix_post, norm_ffn_pre, norm_ffn_post):
    for i in range(DEPTH):
        j = i // N_MIXERS
        h = rms_norm(x, norm_mix_pre[i])
        if i % N_MIXERS == 0:
            h = sliding_window_attention(h, attn_w_qkv[j], attn_b_qkv[j], attn_sinks[j],
                                         attn_w_o[j], attn_b_o[j])
        else:
            h = chunked_sgu(h, sgu_w_in[j], sgu_ln_g[j], sgu_ln_b[j], sgu_w_s[j],
                            sgu_b_s[j], sgu_w_out[j])
        x = x + rms_norm(h, norm_mix_post[i])
        h = conv_gated_ffn(rms_norm(x, norm_ffn_pre[i]), ffn_w_in[i], ffn_conv_w[i],
                           ffn_conv_b[i], ffn_w_out[i])
        x = x + rms_norm(h, norm_ffn_post[i])
    return x
```

```python
import functools

import jax
import jax.numpy as jnp
import numpy as np
from jax import lax
from jax.experimental import pallas as pl
from jax.experimental.pallas import tpu as pltpu

D_MODEL = 1024
SEQ = 4096

N_HEADS = 16
N_KV_HEADS = 4
HEAD_DIM = 64
Q_PER_KV = N_HEADS // N_KV_HEADS
WINDOW = 128
BLOCK = 128
Q_DIM = N_HEADS * HEAD_DIM
KV_DIM = N_KV_HEADS * HEAD_DIM
QKV_DIM = Q_DIM + 2 * KV_DIM
ALIBI_MAX_BIAS = 8.0

CHUNK = 128
SGU_HALF = 3 * D_MODEL
N_SGU_GROUPS = 8
SGU_GROUP_DIM = SGU_HALF // N_SGU_GROUPS

D_FF = 2816
CONV_WIDTH = 3
EPS = 1e-6

V7X_LANES = 128
V7X_SUBLANES = 8
V7X_VMEM_LIMIT_BYTES = 60000 * 1024

ROW_TILE = 512
FFN_COL_CHUNK = 256
SGU_V_CHUNK = 768

BF16 = jnp.bfloat16
F32 = jnp.float32


def _dot(a, b):
    return jnp.dot(a, b, preferred_element_type=F32)


def _rms_norm(x, g_row):
    return x * lax.rsqrt(jnp.mean(x * x, axis=-1, keepdims=True) + EPS) * g_row


def _resident(shape):
    zeros = (0,) * len(shape)
    return pl.BlockSpec(shape, lambda i: zeros, pipeline_mode=pl.Buffered(1))


def _row_tiled(width):
    return pl.BlockSpec((ROW_TILE, width), lambda i: (i, 0))


def _compiler_params():
    return pltpu.CompilerParams(
        dimension_semantics=("arbitrary",),
        vmem_limit_bytes=V7X_VMEM_LIMIT_BYTES,
    )


def _ffn_kernel(x_ref, gpre_ref, gpost_ref, win_ref, cw_ref, cb_ref, wout_ref,
                o_ref, gs_ref, a_ref):
    tiles_per_seq = SEQ // ROW_TILE
    halo = V7X_SUBLANES

    @pl.when(pl.program_id(0) % tiles_per_seq == 0)
    def _():
        gs_ref[0:halo, :] = jnp.zeros((halo, D_FF), F32)

    x = x_ref[...]
    h = _rms_norm(x, gpre_ref[...]).astype(BF16)

    for c in range(D_FF // FFN_COL_CHUNK):
        cols = slice(c * FFN_COL_CHUNK, (c + 1) * FFN_COL_CHUNK)
        ucols = slice(D_FF + c * FFN_COL_CHUNK, D_FF + (c + 1) * FFN_COL_CHUNK)
        g = _dot(h, win_ref[:, cols])
        u = _dot(h, win_ref[:, ucols])
        gs_ref[halo:halo + ROW_TILE, cols] = g
        g1 = gs_ref[halo - 1:halo - 1 + ROW_TILE, cols]
        g2 = gs_ref[halo - 2:halo - 2 + ROW_TILE, cols]
        gs_ref[0:halo, cols] = g[ROW_TILE - halo:, :]
        gc = (cb_ref[:, cols] + cw_ref[0:1, cols] * g2 + cw_ref[1:2, cols] * g1
              + cw_ref[2:3, cols] * g)
        a_ref[:, cols] = (jax.nn.gelu(gc) * u).astype(BF16)

    y = _dot(a_ref[...], wout_ref[...])
    o_ref[...] = x + _rms_norm(y, gpost_ref[...])


def _ffn_layer(x2d, g_pre, g_post, w_in, conv_w, conv_b, w_out):
    n_rows = x2d.shape[0]
    return pl.pallas_call(
        _ffn_kernel,
        out_shape=jax.ShapeDtypeStruct((n_rows, D_MODEL), F32),
        grid=(n_rows // ROW_TILE,),
        in_specs=[
            _row_tiled(D_MODEL),
            _resident((1, D_MODEL)),
            _resident((1, D_MODEL)),
            _resident((D_MODEL, 2 * D_FF)),
            _resident((CONV_WIDTH, D_FF)),
            _resident((1, D_FF)),
            _resident((D_FF, D_MODEL)),
        ],
        out_specs=_row_tiled(D_MODEL),
        scratch_shapes=[
            pltpu.VMEM((ROW_TILE + V7X_SUBLANES, D_FF), F32),
            pltpu.VMEM((ROW_TILE, D_FF), BF16),
        ],
        compiler_params=_compiler_params(),
        name="conv_ffn_layer",
    )(x2d, g_pre.reshape(1, D_MODEL), g_post.reshape(1, D_MODEL),
      w_in.astype(BF16), conv_w, conv_b.reshape(1, D_FF), w_out.astype(BF16))


def _sgu_kernel(x_ref, gpre_ref, gpost_ref, win_ref, lng_ref, lnb_ref, ws_ref,
                bs_ref, wout_ref, o_ref, v_ref, a_ref):
    x = x_ref[...]
    h = _rms_norm(x, gpre_ref[...]).astype(BF16)

    vsum = jnp.zeros((ROW_TILE, 1), F32)
    for c in range(SGU_HALF // SGU_V_CHUNK):
        cols = slice(c * SGU_V_CHUNK, (c + 1) * SGU_V_CHUNK)
        wcols = slice(SGU_HALF + c * SGU_V_CHUNK, SGU_HALF + (c + 1) * SGU_V_CHUNK)
        v = jax.nn.gelu(_dot(h, win_ref[:, wcols]))
        v_ref[:, cols] = v
        vsum = vsum + jnp.sum(v, axis=-1, keepdims=True)
    mu = vsum * (1.0 / SGU_HALF)
    vsq = jnp.zeros((ROW_TILE, 1), F32)
    for c in range(SGU_HALF // SGU_V_CHUNK):
        cols = slice(c * SGU_V_CHUNK, (c + 1) * SGU_V_CHUNK)
        d = v_ref[:, cols] - mu
        vsq = vsq + jnp.sum(d * d, axis=-1, keepdims=True)
    rstd = lax.rsqrt(vsq * (1.0 / SGU_HALF) + EPS)

    row = lax.broadcasted_iota(jnp.int32, (CHUNK, CHUNK), 0)
    col = lax.broadcasted_iota(jnp.int32, (CHUNK, CHUNK), 1)
    causal = row >= col

    for g in range(N_SGU_GROUPS):
        cols = slice(g * SGU_GROUP_DIM, (g + 1) * SGU_GROUP_DIM)
        vn = ((v_ref[:, cols] - mu) * rstd * lng_ref[:, cols]
              + lnb_ref[:, cols]).astype(BF16)
        ws = jnp.where(causal, ws_ref[g], 0.0).astype(BF16)
        bias = bs_ref[:, g * V7X_LANES:(g + 1) * V7X_LANES]
        bias = jnp.concatenate([bias] * (SGU_GROUP_DIM // V7X_LANES), axis=1)
        u = jax.nn.gelu(_dot(h, win_ref[:, cols]))
        for c in range(ROW_TILE // CHUNK):
            rows = slice(c * CHUNK, (c + 1) * CHUNK)
            sv = _dot(ws, vn[rows]) + bias
            a_ref[rows, cols] = (u[rows] * sv).astype(BF16)

    y = _dot(a_ref[...], wout_ref[...])
    o_ref[...] = x + _rms_norm(y, gpost_ref[...])


def _sgu_layer(x2d, g_pre, g_post, w_in, ln_g, ln_b, w_s, b_s, w_out):
    n_rows = x2d.shape[0]
    bs_slab = jnp.broadcast_to(b_s.T[:, :, None],
                               (CHUNK, N_SGU_GROUPS, V7X_LANES)).reshape(CHUNK, -1)
    return pl.pallas_call(
        _sgu_kernel,
        out_shape=jax.ShapeDtypeStruct((n_rows, D_MODEL), F32),
        grid=(n_rows // ROW_TILE,),
        in_specs=[
            _row_tiled(D_MODEL),
            _resident((1, D_MODEL)),
            _resident((1, D_MODEL)),
            _resident((D_MODEL, 2 * SGU_HALF)),
            _resident((1, SGU_HALF)),
            _resident((1, SGU_HALF)),
            _resident((N_SGU_GROUPS, CHUNK, CHUNK)),
            _resident((CHUNK, N_SGU_GROUPS * V7X_LANES)),
            _resident((SGU_HALF, D_MODEL)),
        ],
        out_specs=_row_tiled(D_MODEL),
        scratch_shapes=[
            pltpu.VMEM((ROW_TILE, SGU_HALF), F32),
            pltpu.VMEM((ROW_TILE, SGU_HALF), BF16),
        ],
        compiler_params=_compiler_params(),
        name="sgu_layer",
    )(x2d, g_pre.reshape(1, D_MODEL), g_post.reshape(1, D_MODEL),
      w_in.astype(BF16), ln_g.reshape(1, SGU_HALF), ln_b.reshape(1, SGU_HALF),
      w_s, bs_slab, w_out.astype(BF16))


HALF_LANES = V7X_LANES // 2
KEYS = 2 * BLOCK
PAIR_ROWS = 2 * BLOCK


def _attn_kernel(x_ref, gpre_ref, gpost_ref, wqkv_ref, bqkv_ref, bias_ref,
                 sink_ref, wo_ref, bo_ref, o_ref, q_ref, kpad_ref, vpad_ref, ao_ref):
    tiles_per_seq = SEQ // ROW_TILE
    is_first = pl.program_id(0) % tiles_per_seq == 0

    @pl.when(is_first)
    def _():
        kpad_ref[:, :, 0:BLOCK, :] = jnp.zeros((N_KV_HEADS, 2, BLOCK, V7X_LANES), BF16)
        vpad_ref[:, :, 0:BLOCK, :] = jnp.zeros((N_KV_HEADS, 2, BLOCK, V7X_LANES), BF16)

    x = x_ref[...]
    h = _rms_norm(x, gpre_ref[...]).astype(BF16)

    q = _dot(h, wqkv_ref[:, 0:Q_DIM]) + bqkv_ref[:, 0:Q_DIM]
    q_ref[...] = (q * (HEAD_DIM ** -0.5)).astype(BF16)

    lane = lax.broadcasted_iota(jnp.int32, (ROW_TILE, V7X_LANES), 1)
    lo = lane < HALF_LANES
    for which, pad_ref in ((0, kpad_ref), (1, vpad_ref)):
        base = Q_DIM + which * KV_DIM
        t = _dot(h, wqkv_ref[:, base:base + KV_DIM]) + bqkv_ref[:, base:base + KV_DIM]
        for j in range(KV_DIM // V7X_LANES):
            slab = t[:, j * V7X_LANES:(j + 1) * V7X_LANES]
            swapped = pltpu.roll(slab, HALF_LANES, axis=1)
            rows = slice(BLOCK, BLOCK + ROW_TILE)
            pad_ref[2 * j, 0, rows, :] = jnp.where(lo, slab, 0.0).astype(BF16)
            pad_ref[2 * j, 1, rows, :] = jnp.where(lo, 0.0, swapped).astype(BF16)
            pad_ref[2 * j + 1, 0, rows, :] = jnp.where(lo, swapped, 0.0).astype(BF16)
            pad_ref[2 * j + 1, 1, rows, :] = jnp.where(lo, 0.0, slab).astype(BF16)

    kcol = lax.broadcasted_iota(jnp.int32, (1, 2 * KEYS), 1)
    is_prev_key = (kcol % KEYS) < BLOCK
    first_mask = jnp.where(jnp.logical_and(is_prev_key, is_first), -jnp.inf, 0.0)
    out_lo = lax.broadcasted_iota(jnp.int32, (PAIR_ROWS, V7X_LANES), 1) < HALF_LANES

    for b in range(ROW_TILE // BLOCK):
        qrows = slice(b * BLOCK, (b + 1) * BLOCK)
        krows = slice(b * BLOCK, b * BLOCK + KEYS)
        for g in range(N_KV_HEADS):
            q2 = jnp.concatenate(
                [q_ref[qrows, (2 * g) * V7X_LANES:(2 * g + 1) * V7X_LANES],
                 q_ref[qrows, (2 * g + 1) * V7X_LANES:(2 * g + 2) * V7X_LANES]], axis=0)
            kcat = jnp.concatenate([kpad_ref[g, 0, krows, :], kpad_ref[g, 1, krows, :]], axis=0)
            vcat = jnp.concatenate([vpad_ref[g, 0, krows, :], vpad_ref[g, 1, krows, :]], axis=0)
            s = lax.dot_general(q2, kcat, (((1,), (1,)), ((), ())),
                                preferred_element_type=F32)
            s = s + bias_ref[g]
            if b == 0:
                s = s + first_mask
            probs = []
            inv = []
            for e in range(2):
                se = s[:, e * KEYS:(e + 1) * KEYS]
                sink = sink_ref[g, :, e * V7X_LANES:e * V7X_LANES + 1]
                m = jnp.maximum(jnp.max(se, axis=-1, keepdims=True), sink)
                p = jnp.exp(se - m)
                denom = jnp.sum(p, axis=-1, keepdims=True) + jnp.exp(sink - m)
                probs.append(p.astype(BF16))
                inv.append(1.0 / denom)
            o2 = _dot(jnp.concatenate(probs, axis=1), vcat)
            o2 = o2 * jnp.where(out_lo, inv[0], inv[1])
            ao_ref[qrows, (2 * g) * V7X_LANES:(2 * g + 1) * V7X_LANES] = o2[0:BLOCK].astype(BF16)
            ao_ref[qrows, (2 * g + 1) * V7X_LANES:(2 * g + 2) * V7X_LANES] = o2[BLOCK:].astype(BF16)

    kpad_ref[:, :, 0:BLOCK, :] = kpad_ref[:, :, ROW_TILE:ROW_TILE + BLOCK, :]
    vpad_ref[:, :, 0:BLOCK, :] = vpad_ref[:, :, ROW_TILE:ROW_TILE + BLOCK, :]

    y = _dot(ao_ref[...], wo_ref[...]) + bo_ref[...]
    o_ref[...] = x + _rms_norm(y, gpost_ref[...])


def _attn_tables(sinks):
    hh = jnp.arange(1, N_HEADS + 1, dtype=F32)
    slopes = jnp.exp2(-ALIBI_MAX_BIAS * hh / N_HEADS)
    qi = jnp.arange(BLOCK)[:, None]
    kj = jnp.arange(KEYS)[None, :]
    dist = qi + BLOCK - kj
    valid = (dist >= 0) & (dist < WINDOW)
    bias_h = jnp.where(valid[None], -(slopes[:, None, None] * dist.astype(F32)[None]),
                       -jnp.inf)
    bias = bias_h.reshape(N_KV_HEADS, 2, 2, BLOCK, KEYS).transpose(0, 1, 3, 2, 4)
    bias = bias.reshape(N_KV_HEADS, PAIR_ROWS, 2 * KEYS)
    sink = sinks.astype(F32).reshape(N_KV_HEADS, 2, 1, 2, 1)
    sink = jnp.broadcast_to(sink, (N_KV_HEADS, 2, BLOCK, 2, V7X_LANES))
    sink = sink.reshape(N_KV_HEADS, PAIR_ROWS, 2 * V7X_LANES)
    return bias, sink


def _attn_layer(x2d, g_pre, g_post, w_qkv, b_qkv, sinks, w_o, b_o):
    n_rows = x2d.shape[0]
    bias, sink = _attn_tables(sinks)
    return pl.pallas_call(
        _attn_kernel,
        out_shape=jax.ShapeDtypeStruct((n_rows, D_MODEL), F32),
        grid=(n_rows // ROW_TILE,),
        in_specs=[
            _row_tiled(D_MODEL),
            _resident((1, D_MODEL)),
            _resident((1, D_MODEL)),
            _resident((D_MODEL, QKV_DIM)),
            _resident((1, QKV_DIM)),
            _resident((N_KV_HEADS, PAIR_ROWS, 2 * KEYS)),
            _resident((N_KV_HEADS, PAIR_ROWS, 2 * V7X_LANES)),
            _resident((Q_DIM, D_MODEL)),
            _resident((1, D_MODEL)),
        ],
        out_specs=_row_tiled(D_MODEL),
        scratch_shapes=[
            pltpu.VMEM((ROW_TILE, Q_DIM), BF16),
            pltpu.VMEM((N_KV_HEADS, 2, BLOCK + ROW_TILE, V7X_LANES), BF16),
            pltpu.VMEM((N_KV_HEADS, 2, BLOCK + ROW_TILE, V7X_LANES), BF16),
            pltpu.VMEM((ROW_TILE, Q_DIM), BF16),
        ],
        compiler_params=_compiler_params(),
        name="swa_layer",
    )(x2d, g_pre.reshape(1, D_MODEL), g_post.reshape(1, D_MODEL),
      w_qkv.astype(BF16), b_qkv.reshape(1, QKV_DIM), bias, sink,
      w_o.astype(BF16), b_o.reshape(1, D_MODEL))


def kernel(x, attn_w_qkv, attn_b_qkv, attn_sinks, attn_w_o, attn_b_o, sgu_w_in, sgu_ln_g, sgu_ln_b, sgu_w_s, sgu_b_s, sgu_w_out, ffn_w_in, ffn_conv_w, ffn_conv_b, ffn_w_out, norm_mix_pre, norm_mix_post, norm_ffn_pre, norm_ffn_post):
    batch, seq, d_model = x.shape
    assert (seq, d_model) == (SEQ, D_MODEL) and SEQ % ROW_TILE == 0
    depth = ffn_w_in.shape[0]
    x2d = x.reshape(batch * seq, d_model)
    for i in range(depth):
        j = i // 2
        if i % 2 == 0:
            x2d = _attn_layer(x2d, norm_mix_pre[i], norm_mix_post[i], attn_w_qkv[j],
                              attn_b_qkv[j], attn_sinks[j], attn_w_o[j], attn_b_o[j])
        else:
            x2d = _sgu_layer(x2d, norm_mix_pre[i], norm_mix_post[i], sgu_w_in[j],
                             sgu_ln_g[j], sgu_ln_b[j], sgu_w_s[j], sgu_b_s[j], sgu_w_out[j])
        x2d = _ffn_layer(x2d, norm_ffn_pre[i], norm_ffn_post[i], ffn_w_in[i],
                         ffn_conv_w[i], ffn_conv_b[i], ffn_w_out[i])
    return x2d.reshape(batch, seq, d_model)
```

```python
import functools

import jax
import jax.numpy as jnp
import numpy as np
from jax import lax
from jax.experimental import pallas as pl
from jax.experimental.pallas import tpu as pltpu

D_MODEL = 1024
SEQ = 4096

N_HEADS = 16
N_KV_HEADS = 4
HEAD_DIM = 64
Q_PER_KV = N_HEADS // N_KV_HEADS
WINDOW = 128
BLOCK = 128
Q_DIM = N_HEADS * HEAD_DIM
KV_DIM = N_KV_HEADS * HEAD_DIM
QKV_DIM = Q_DIM + 2 * KV_DIM
ALIBI_MAX_BIAS = 8.0

CHUNK = 128
SGU_HALF = 3 * D_MODEL
N_SGU_GROUPS = 8
SGU_GROUP_DIM = SGU_HALF // N_SGU_GROUPS

D_FF = 2816
CONV_WIDTH = 3
EPS = 1e-6

V7X_LANES = 128
V7X_SUBLANES = 8
V7X_VMEM_LIMIT_BYTES = 60000 * 1024

ROW_TILE = 512
FFN_COL_CHUNK = 256
SGU_V_CHUNK = 768

BF16 = jnp.bfloat16
F32 = jnp.float32


def _dot(a, b):
    return jnp.dot(a, b, preferred_element_type=F32)


def _rms_norm(x, g_row):
    return x * lax.rsqrt(jnp.mean(x * x, axis=-1, keepdims=True) + EPS) * g_row


def _resident(shape):
    zeros = (0,) * len(shape)
    return pl.BlockSpec(shape, lambda i: zeros, pipeline_mode=pl.Buffered(1))


def _row_tiled(width):
    return pl.BlockSpec((ROW_TILE, width), lambda i: (i, 0))


def _compiler_params():
    return pltpu.CompilerParams(
        dimension_semantics=("arbitrary",),
        vmem_limit_bytes=V7X_VMEM_LIMIT_BYTES,
    )


def _ffn_kernel(x_ref, gpre_ref, gpost_ref, win_ref, cw_ref, cb_ref, wout_ref,
                o_ref, gs_ref, a_ref):
    tiles_per_seq = SEQ // ROW_TILE
    halo = V7X_SUBLANES

    @pl.when(pl.program_id(0) % tiles_per_seq == 0)
    def _():
        gs_ref[0:halo, :] = jnp.zeros((halo, D_FF), F32)

    x = x_ref[...]
    h = _rms_norm(x, gpre_ref[...]).astype(BF16)

    for c in range(D_FF // FFN_COL_CHUNK):
        cols = slice(c * FFN_COL_CHUNK, (c + 1) * FFN_COL_CHUNK)
        ucols = slice(D_FF + c * FFN_COL_CHUNK, D_FF + (c + 1) * FFN_COL_CHUNK)
        g = _dot(h, win_ref[:, cols])
        u = _dot(h, win_ref[:, ucols])
        gs_ref[halo:halo + ROW_TILE, cols] = g
        g1 = gs_ref[halo - 1:halo - 1 + ROW_TILE, cols]
        g2 = gs_ref[halo - 2:halo - 2 + ROW_TILE, cols]
        gs_ref[0:halo, cols] = g[ROW_TILE - halo:, :]
        gc = (cb_ref[:, cols] + cw_ref[0:1, cols] * g2 + cw_ref[1:2, cols] * g1
              + cw_ref[2:3, cols] * g)
        a_ref[:, cols] = (jax.nn.gelu(gc) * u).astype(BF16)

    y = _dot(a_ref[...], wout_ref[...])
    o_ref[...] = x + _rms_norm(y, gpost_ref[...])


def _ffn_layer(x2d, g_pre, g_post, w_in, conv_w, conv_b, w_out):
    n_rows = x2d.shape[0]
    return pl.pallas_call(
        _ffn_kernel,
        out_shape=jax.ShapeDtypeStruct((n_rows, D_MODEL), F32),
        grid=(n_rows // ROW_TILE,),
        in_specs=[
            _row_tiled(D_MODEL),
            _resident((1, D_MODEL)),
            _resident((1, D_MODEL)),
            _resident((D_MODEL, 2 * D_FF)),
            _resident((CONV_WIDTH, D_FF)),
            _resident((1, D_FF)),
            _resident((D_FF, D_MODEL)),
        ],
        out_specs=_row_tiled(D_MODEL),
        scratch_shapes=[
            pltpu.VMEM((ROW_TILE + V7X_SUBLANES, D_FF), F32),
            pltpu.VMEM((ROW_TILE, D_FF), BF16),
        ],
        compiler_params=_compiler_params(),
        name="conv_ffn_layer",
    )(x2d, g_pre.reshape(1, D_MODEL), g_post.reshape(1, D_MODEL),
      w_in.astype(BF16), conv_w, conv_b.reshape(1, D_FF), w_out.astype(BF16))


def _sgu_kernel(x_ref, gpre_ref, gpost_ref, win_ref, lng_ref, lnb_ref, ws_ref,
                bs_ref, wout_ref, o_ref, v_ref, a_ref):
    x = x_ref[...]
    h = _rms_norm(x, gpre_ref[...]).astype(BF16)

    vsum = jnp.zeros((ROW_TILE, 1), F32)
    for c in range(SGU_HALF // SGU_V_CHUNK):
        cols = slice(c * SGU_V_CHUNK, (c + 1) * SGU_V_CHUNK)
        wcols = slice(SGU_HALF + c * SGU_V_CHUNK, SGU_HALF + (c + 1) * SGU_V_CHUNK)
        v = jax.nn.gelu(_dot(h, win_ref[:, wcols]))
        v_ref[:, cols] = v
        vsum = vsum + jnp.sum(v, axis=-1, keepdims=True)
    mu = vsum * (1.0 / SGU_HALF)
    vsq = jnp.zeros((ROW_TILE, 1), F32)
    for c in range(SGU_HALF // SGU_V_CHUNK):
        cols = slice(c * SGU_V_CHUNK, (c + 1) * SGU_V_CHUNK)
        d = v_ref[:, cols] - mu
        vsq = vsq + jnp.sum(d * d, axis=-1, keepdims=True)
    rstd = lax.rsqrt(vsq * (1.0 / SGU_HALF) + EPS)

    row = lax.broadcasted_iota(jnp.int32, (CHUNK, CHUNK), 0)
    col = lax.broadcasted_iota(jnp.int32, (CHUNK, CHUNK), 1)
    causal = row >= col

    for g in range(N_SGU_GROUPS):
        cols = slice(g * SGU_GROUP_DIM, (g + 1) * SGU_GROUP_DIM)
        vn = ((v_ref[:, cols] - mu) * rstd * lng_ref[:, cols]
              + lnb_ref[:, cols]).astype(BF16)
        ws = jnp.where(causal, ws_ref[g], 0.0).astype(BF16)
        bias = bs_ref[:, g * V7X_LANES:(g + 1) * V7X_LANES]
        bias = jnp.concatenate([bias] * (SGU_GROUP_DIM // V7X_LANES), axis=1)
        u = jax.nn.gelu(_dot(h, win_ref[:, cols]))
        for c in range(ROW_TILE // CHUNK):
            rows = slice(c * CHUNK, (c + 1) * CHUNK)
            sv = _dot(ws, vn[rows]) + bias
            a_ref[rows, cols] = (u[rows] * sv).astype(BF16)

    y = _dot(a_ref[...], wout_ref[...])
    o_ref[...] = x + _rms_norm(y, gpost_ref[...])


def _sgu_layer(x2d, g_pre, g_post, w_in, ln_g, ln_b, w_s, b_s, w_out):
    n_rows = x2d.shape[0]
    bs_slab = jnp.broadcast_to(b_s.T[:, :, None],
                               (CHUNK, N_SGU_GROUPS, V7X_LANES)).reshape(CHUNK, -1)
    return pl.pallas_call(
        _sgu_kernel,
        out_shape=jax.ShapeDtypeStruct((n_rows, D_MODEL), F32),
        grid=(n_rows // ROW_TILE,),
        in_specs=[
            _row_tiled(D_MODEL),
            _resident((1, D_MODEL)),
            _resident((1, D_MODEL)),
            _resident((D_MODEL, 2 * SGU_HALF)),
            _resident((1, SGU_HALF)),
            _resident((1, SGU_HALF)),
            _resident((N_SGU_GROUPS, CHUNK, CHUNK)),
            _resident((CHUNK, N_SGU_GROUPS * V7X_LANES)),
            _resident((SGU_HALF, D_MODEL)),
        ],
        out_specs=_row_tiled(D_MODEL),
        scratch_shapes=[
            pltpu.VMEM((ROW_TILE, SGU_HALF), F32),
            pltpu.VMEM((ROW_TILE, SGU_HALF), BF16),
        ],
        compiler_params=_compiler_params(),
        name="sgu_layer",
    )(x2d, g_pre.reshape(1, D_MODEL), g_post.reshape(1, D_MODEL),
      w_in.astype(BF16), ln_g.reshape(1, SGU_HALF), ln_b.reshape(1, SGU_HALF),
      w_s, bs_slab, w_out.astype(BF16))


HALF_LANES = V7X_LANES // 2
KEYS = 2 * BLOCK
PAIR_COLS = 2 * BLOCK


def _attn_kernel(x_ref, gpre_ref, gpost_ref, wqkv_ref, bqkv_ref, bias_ref,
                 sink_ref, wo_ref, bo_ref, o_ref, q_ref, kpad_ref, vt_ref, ao_ref):
    tiles_per_seq = SEQ // ROW_TILE
    is_first = pl.program_id(0) % tiles_per_seq == 0

    @pl.when(is_first)
    def _():
        kpad_ref[:, :, 0:BLOCK, :] = jnp.zeros((N_KV_HEADS, 2, BLOCK, V7X_LANES), BF16)
        vt_ref[:, :, :, 0:BLOCK] = jnp.zeros((N_KV_HEADS, 2, V7X_LANES, BLOCK), BF16)

    x = x_ref[...]
    h = _rms_norm(x, gpre_ref[...]).astype(BF16)

    q = _dot(h, wqkv_ref[:, 0:Q_DIM]) + bqkv_ref[:, 0:Q_DIM]
    q_ref[...] = (q * (HEAD_DIM ** -0.5)).astype(BF16)

    new_rows = slice(BLOCK, BLOCK + ROW_TILE)
    k = _dot(h, wqkv_ref[:, Q_DIM:Q_DIM + KV_DIM]) + bqkv_ref[:, Q_DIM:Q_DIM + KV_DIM]
    lo = lax.broadcasted_iota(jnp.int32, (ROW_TILE, V7X_LANES), 1) < HALF_LANES
    for j in range(KV_DIM // V7X_LANES):
        slab = k[:, j * V7X_LANES:(j + 1) * V7X_LANES]
        swapped = pltpu.roll(slab, HALF_LANES, axis=1)
        kpad_ref[2 * j, 0, new_rows, :] = jnp.where(lo, slab, 0.0).astype(BF16)
        kpad_ref[2 * j, 1, new_rows, :] = jnp.where(lo, 0.0, swapped).astype(BF16)
        kpad_ref[2 * j + 1, 0, new_rows, :] = jnp.where(lo, swapped, 0.0).astype(BF16)
        kpad_ref[2 * j + 1, 1, new_rows, :] = jnp.where(lo, 0.0, slab).astype(BF16)

    v = _dot(h, wqkv_ref[:, Q_DIM + KV_DIM:QKV_DIM]) + bqkv_ref[:, Q_DIM + KV_DIM:QKV_DIM]
    zeros_half = jnp.zeros((HALF_LANES, ROW_TILE), BF16)
    for j in range(KV_DIM // V7X_LANES):
        slab_t = v[:, j * V7X_LANES:(j + 1) * V7X_LANES].T.astype(BF16)
        for half in range(2):
            head_t = slab_t[half * HALF_LANES:(half + 1) * HALF_LANES]
            vt_ref[2 * j + half, 0, :, new_rows] = jnp.concatenate([head_t, zeros_half], axis=0)
            vt_ref[2 * j + half, 1, :, new_rows] = jnp.concatenate([zeros_half, head_t], axis=0)

    krow = lax.broadcasted_iota(jnp.int32, (2 * KEYS, PAIR_COLS), 0)
    drop_prev = jnp.logical_and((krow % KEYS) < BLOCK, is_first)

    for b in range(ROW_TILE // BLOCK):
        qrows = slice(b * BLOCK, (b + 1) * BLOCK)
        krows = slice(b * BLOCK, b * BLOCK + KEYS)
        for g in range(N_KV_HEADS):
            q2 = jnp.concatenate(
                [q_ref[qrows, (2 * g) * V7X_LANES:(2 * g + 1) * V7X_LANES],
                 q_ref[qrows, (2 * g + 1) * V7X_LANES:(2 * g + 2) * V7X_LANES]], axis=0)
            kcat = jnp.concatenate([kpad_ref[g, 0, krows, :], kpad_ref[g, 1, krows, :]], axis=0)
            st = lax.dot_general(kcat, q2, (((1,), (1,)), ((), ())),
                                 preferred_element_type=F32)
            bias = bias_ref[g]
            if b == 0:
                bias = jnp.where(drop_prev, -jnp.inf, bias)
            st = st + bias
            probs = []
            for e in range(2):
                se = st[e * KEYS:(e + 1) * KEYS]
                sink = sink_ref[2 * g + e:2 * g + e + 1, :]
                m = jnp.maximum(jnp.max(se, axis=0, keepdims=True), sink)
                p = jnp.exp(se - m)
                denom = jnp.sum(p, axis=0, keepdims=True) + jnp.exp(sink - m)
                probs.append((p * (1.0 / denom)).astype(BF16))
            vcat_t = jnp.concatenate([vt_ref[g, 0, :, krows], vt_ref[g, 1, :, krows]], axis=1)
            ot = _dot(vcat_t, jnp.concatenate(probs, axis=0))
            o2 = ot.T.astype(BF16)
            ao_ref[qrows, (2 * g) * V7X_LANES:(2 * g + 1) * V7X_LANES] = o2[0:BLOCK]
            ao_ref[qrows, (2 * g + 1) * V7X_LANES:(2 * g + 2) * V7X_LANES] = o2[BLOCK:]

    kpad_ref[:, :, 0:BLOCK, :] = kpad_ref[:, :, ROW_TILE:ROW_TILE + BLOCK, :]
    vt_ref[:, :, :, 0:BLOCK] = vt_ref[:, :, :, ROW_TILE:ROW_TILE + BLOCK]

    y = _dot(ao_ref[...], wo_ref[...]) + bo_ref[...]
    o_ref[...] = x + _rms_norm(y, gpost_ref[...])


def _attn_tables(sinks):
    hh = jnp.arange(1, N_HEADS + 1, dtype=F32)
    slopes = jnp.exp2(-ALIBI_MAX_BIAS * hh / N_HEADS)
    qi = jnp.arange(BLOCK)[:, None]
    kj = jnp.arange(KEYS)[None, :]
    dist = qi + BLOCK - kj
    valid = (dist >= 0) & (dist < WINDOW)
    bias_h = jnp.where(valid[None], -(slopes[:, None, None] * dist.astype(F32)[None]),
                       -jnp.inf)
    bias = bias_h.reshape(N_KV_HEADS, 2, 2, BLOCK, KEYS).transpose(0, 2, 4, 1, 3)
    bias = bias.reshape(N_KV_HEADS, 2 * KEYS, PAIR_COLS)
    sink = sinks.astype(F32).reshape(N_KV_HEADS, 2, 2).transpose(0, 2, 1)
    sink = jnp.broadcast_to(sink[..., None], (N_KV_HEADS, 2, 2, BLOCK))
    return bias, sink.reshape(2 * N_KV_HEADS, PAIR_COLS)


def _attn_layer(x2d, g_pre, g_post, w_qkv, b_qkv, sinks, w_o, b_o):
    n_rows = x2d.shape[0]
    bias, sink = _attn_tables(sinks)
    return pl.pallas_call(
        _attn_kernel,
        out_shape=jax.ShapeDtypeStruct((n_rows, D_MODEL), F32),
        grid=(n_rows // ROW_TILE,),
        in_specs=[
            _row_tiled(D_MODEL),
            _resident((1, D_MODEL)),
            _resident((1, D_MODEL)),
            _resident((D_MODEL, QKV_DIM)),
            _resident((1, QKV_DIM)),
            _resident((N_KV_HEADS, 2 * KEYS, PAIR_COLS)),
            _resident((2 * N_KV_HEADS, PAIR_COLS)),
            _resident((Q_DIM, D_MODEL)),
            _resident((1, D_MODEL)),
        ],
        out_specs=_row_tiled(D_MODEL),
        scratch_shapes=[
            pltpu.VMEM((ROW_TILE, Q_DIM), BF16),
            pltpu.VMEM((N_KV_HEADS, 2, BLOCK + ROW_TILE, V7X_LANES), BF16),
            pltpu.VMEM((N_KV_HEADS, 2, V7X_LANES, BLOCK + ROW_TILE), BF16),
            pltpu.VMEM((ROW_TILE, Q_DIM), BF16),
        ],
        compiler_params=_compiler_params(),
        name="swa_layer",
    )(x2d, g_pre.reshape(1, D_MODEL), g_post.reshape(1, D_MODEL),
      w_qkv.astype(BF16), b_qkv.reshape(1, QKV_DIM), bias, sink,
      w_o.astype(BF16), b_o.reshape(1, D_MODEL))


def kernel(x, attn_w_qkv, attn_b_qkv, attn_sinks, attn_w_o, attn_b_o, sgu_w_in, sgu_ln_g, sgu_ln_b, sgu_w_s, sgu_b_s, sgu_w_out, ffn_w_in, ffn_conv_w, ffn_conv_b, ffn_w_out, norm_mix_pre, norm_mix_post, norm_ffn_pre, norm_ffn_post):
    batch, seq, d_model = x.shape
    assert (seq, d_model) == (SEQ, D_MODEL) and SEQ % ROW_TILE == 0
    depth = ffn_w_in.shape[0]
    x2d = x.reshape(batch * seq, d_model)
    for i in range(depth):
        j = i // 2
        if i % 2 == 0:
            x2d = _attn_layer(x2d, norm_mix_pre[i], norm_mix_post[i], attn_w_qkv[j],
                              attn_b_qkv[j], attn_sinks[j], attn_w_o[j], attn_b_o[j])
        else:
            x2d = _sgu_layer(x2d, norm_mix_pre[i], norm_mix_post[i], sgu_w_in[j],
                             sgu_ln_g[j], sgu_ln_b[j], sgu_w_s[j], sgu_b_s[j], sgu_w_out[j])
        x2d = _ffn_layer(x2d, norm_ffn_pre[i], norm_ffn_post[i], ffn_w_in[i],
                         ffn_conv_w[i], ffn_conv_b[i], ffn_w_out[i])
    return x2d.reshape(batch, seq, d_model)
```

```python
import functools

import jax
import jax.numpy as jnp
import numpy as np
from jax import lax
from jax.experimental import pallas as pl
from jax.experimental.pallas import tpu as pltpu

D_MODEL = 1024
SEQ = 4096

N_HEADS = 16
N_KV_HEADS = 4
HEAD_DIM = 64
Q_PER_KV = N_HEADS // N_KV_HEADS
WINDOW = 128
BLOCK = 128
Q_DIM = N_HEADS * HEAD_DIM
KV_DIM = N_KV_HEADS * HEAD_DIM
QKV_DIM = Q_DIM + 2 * KV_DIM
ALIBI_MAX_BIAS = 8.0

CHUNK = 128
SGU_HALF = 3 * D_MODEL
N_SGU_GROUPS = 8
SGU_GROUP_DIM = SGU_HALF // N_SGU_GROUPS

D_FF = 2816
CONV_WIDTH = 3
EPS = 1e-6

V7X_LANES = 128
V7X_SUBLANES = 8
V7X_VMEM_LIMIT_BYTES = 60000 * 1024

ROW_TILE = 512
FFN_COL_CHUNK = 256
SGU_V_CHUNK = 768

BF16 = jnp.bfloat16
F32 = jnp.float32


def _dot(a, b):
    return jnp.dot(a, b, preferred_element_type=F32)


def _rms_norm(x, g_row):
    return x * lax.rsqrt(jnp.mean(x * x, axis=-1, keepdims=True) + EPS) * g_row


def _resident(shape):
    zeros = (0,) * len(shape)
    return pl.BlockSpec(shape, lambda i: zeros, pipeline_mode=pl.Buffered(1))


def _resident_layer(shape, layer):
    index = (layer,) + (0,) * len(shape)
    return pl.BlockSpec((None,) + tuple(shape), lambda i: index,
                        pipeline_mode=pl.Buffered(1))


def _row_tiled(width):
    return pl.BlockSpec((ROW_TILE, width), lambda i: (i, 0))


def _compiler_params():
    return pltpu.CompilerParams(
        dimension_semantics=("arbitrary",),
        vmem_limit_bytes=V7X_VMEM_LIMIT_BYTES,
    )


def _ffn_kernel(x_ref, gpre_ref, gpost_ref, win_ref, cw_ref, cb_ref, wout_ref,
                o_ref, gs_ref, a_ref):
    tiles_per_seq = SEQ // ROW_TILE
    halo = V7X_SUBLANES

    @pl.when(pl.program_id(0) % tiles_per_seq == 0)
    def _():
        gs_ref[0:halo, :] = jnp.zeros((halo, D_FF), F32)

    x = x_ref[...]
    h = _rms_norm(x, gpre_ref[...]).astype(BF16)

    for c in range(D_FF // FFN_COL_CHUNK):
        cols = slice(c * FFN_COL_CHUNK, (c + 1) * FFN_COL_CHUNK)
        ucols = slice(D_FF + c * FFN_COL_CHUNK, D_FF + (c + 1) * FFN_COL_CHUNK)
        g = _dot(h, win_ref[:, cols])
        u = _dot(h, win_ref[:, ucols])
        gs_ref[halo:halo + ROW_TILE, cols] = g
        g1 = gs_ref[halo - 1:halo - 1 + ROW_TILE, cols]
        g2 = gs_ref[halo - 2:halo - 2 + ROW_TILE, cols]
        gs_ref[0:halo, cols] = g[ROW_TILE - halo:, :]
        gc = (cb_ref[:, cols] + cw_ref[0:1, cols] * g2 + cw_ref[1:2, cols] * g1
              + cw_ref[2:3, cols] * g)
        a_ref[:, cols] = (jax.nn.gelu(gc) * u).astype(BF16)

    y = _dot(a_ref[...], wout_ref[...])
    o_ref[...] = x + _rms_norm(y, gpost_ref[...])


def _ffn_layer(x2d, layer, g_pre, g_post, w_in_stack, conv_w, conv_b, w_out_stack):
    n_rows = x2d.shape[0]
    return pl.pallas_call(
        _ffn_kernel,
        out_shape=jax.ShapeDtypeStruct((n_rows, D_MODEL), F32),
        grid=(n_rows // ROW_TILE,),
        in_specs=[
            _row_tiled(D_MODEL),
            _resident((1, D_MODEL)),
            _resident((1, D_MODEL)),
            _resident_layer((D_MODEL, 2 * D_FF), layer),
            _resident((CONV_WIDTH, D_FF)),
            _resident((1, D_FF)),
            _resident_layer((D_FF, D_MODEL), layer),
        ],
        out_specs=_row_tiled(D_MODEL),
        scratch_shapes=[
            pltpu.VMEM((ROW_TILE + V7X_SUBLANES, D_FF), F32),
            pltpu.VMEM((ROW_TILE, D_FF), BF16),
        ],
        compiler_params=_compiler_params(),
        name="conv_ffn_layer",
    )(x2d, g_pre.reshape(1, D_MODEL), g_post.reshape(1, D_MODEL),
      w_in_stack, conv_w, conv_b.reshape(1, D_FF), w_out_stack)


def _sgu_kernel(x_ref, gpre_ref, gpost_ref, win_ref, lng_ref, lnb_ref, ws_ref,
                bs_ref, wout_ref, o_ref, v_ref, a_ref):
    x = x_ref[...]
    h = _rms_norm(x, gpre_ref[...]).astype(BF16)

    vsum = jnp.zeros((ROW_TILE, 1), F32)
    for c in range(SGU_HALF // SGU_V_CHUNK):
        cols = slice(c * SGU_V_CHUNK, (c + 1) * SGU_V_CHUNK)
        wcols = slice(SGU_HALF + c * SGU_V_CHUNK, SGU_HALF + (c + 1) * SGU_V_CHUNK)
        v = jax.nn.gelu(_dot(h, win_ref[:, wcols]))
        v_ref[:, cols] = v
        vsum = vsum + jnp.sum(v, axis=-1, keepdims=True)
    mu = vsum * (1.0 / SGU_HALF)

    pair_dim = 2 * SGU_GROUP_DIM
    n_pairs = N_SGU_GROUPS // 2
    lanes_per_group = SGU_GROUP_DIM // V7X_LANES

    def value_half(pair):
        return jax.nn.gelu(_dot(h, win_ref[:, pair * pair_dim:(pair + 1) * pair_dim]))

    u_next = value_half(0)

    vsq = jnp.zeros((ROW_TILE, 1), F32)
    for c in range(SGU_HALF // SGU_V_CHUNK):
        cols = slice(c * SGU_V_CHUNK, (c + 1) * SGU_V_CHUNK)
        d = v_ref[:, cols] - mu
        vsq = vsq + jnp.sum(d * d, axis=-1, keepdims=True)
    rstd = lax.rsqrt(vsq * (1.0 / SGU_HALF) + EPS)

    row = lax.broadcasted_iota(jnp.int32, (CHUNK, CHUNK), 0)
    col = lax.broadcasted_iota(jnp.int32, (CHUNK, CHUNK), 1)
    causal = row >= col
    zeros_group = jnp.zeros((CHUNK, SGU_GROUP_DIM), BF16)

    for pair in range(n_pairs):
        cols = slice(pair * pair_dim, (pair + 1) * pair_dim)
        u = u_next
        if pair + 1 < n_pairs:
            u_next = value_half(pair + 1)
        vn = ((v_ref[:, cols] - mu) * rstd * lng_ref[:, cols]
              + lnb_ref[:, cols]).astype(BF16)
        ws = jnp.concatenate(
            [jnp.where(causal, ws_ref[2 * pair + k], 0.0).astype(BF16) for k in range(2)],
            axis=1)
        bias = jnp.concatenate(
            [bs_ref[:, (2 * pair + k) * V7X_LANES:(2 * pair + k + 1) * V7X_LANES]
             for k in range(2) for _ in range(lanes_per_group)], axis=1)
        for c in range(ROW_TILE // CHUNK):
            rows = slice(c * CHUNK, (c + 1) * CHUNK)
            vc = vn[rows]
            blockdiag = jnp.concatenate(
                [jnp.concatenate([vc[:, :SGU_GROUP_DIM], zeros_group], axis=1),
                 jnp.concatenate([zeros_group, vc[:, SGU_GROUP_DIM:]], axis=1)], axis=0)
            sv = _dot(ws, blockdiag) + bias
            a_ref[rows, cols] = (u[rows] * sv).astype(BF16)

    y = _dot(a_ref[...], wout_ref[...])
    o_ref[...] = x + _rms_norm(y, gpost_ref[...])


def _sgu_layer(x2d, layer, g_pre, g_post, w_in_stack, ln_g, ln_b, w_s, b_s, w_out_stack):
    n_rows = x2d.shape[0]
    bs_slab = jnp.broadcast_to(b_s.T[:, :, None],
                               (CHUNK, N_SGU_GROUPS, V7X_LANES)).reshape(CHUNK, -1)
    return pl.pallas_call(
        _sgu_kernel,
        out_shape=jax.ShapeDtypeStruct((n_rows, D_MODEL), F32),
        grid=(n_rows // ROW_TILE,),
        in_specs=[
            _row_tiled(D_MODEL),
            _resident((1, D_MODEL)),
            _resident((1, D_MODEL)),
            _resident_layer((D_MODEL, 2 * SGU_HALF), layer),
            _resident((1, SGU_HALF)),
            _resident((1, SGU_HALF)),
            _resident((N_SGU_GROUPS, CHUNK, CHUNK)),
            _resident((CHUNK, N_SGU_GROUPS * V7X_LANES)),
            _resident_layer((SGU_HALF, D_MODEL), layer),
        ],
        out_specs=_row_tiled(D_MODEL),
        scratch_shapes=[
            pltpu.VMEM((ROW_TILE, SGU_HALF), F32),
            pltpu.VMEM((ROW_TILE, SGU_HALF), BF16),
        ],
        compiler_params=_compiler_params(),
        name="sgu_layer",
    )(x2d, g_pre.reshape(1, D_MODEL), g_post.reshape(1, D_MODEL),
      w_in_stack, ln_g.reshape(1, SGU_HALF), ln_b.reshape(1, SGU_HALF),
      w_s, bs_slab, w_out_stack)


LOG2_E = float(np.log2(np.e))
HALF_LANES = V7X_LANES // 2
KEYS = 2 * BLOCK
PAIR_COLS = 2 * BLOCK


def _attn_kernel(x_ref, gpre_ref, gpost_ref, wqkv_ref, bqkv_ref, bias_ref,
                 sink_ref, wo_ref, bo_ref, o_ref, q_ref, kpad_ref, vt_ref, ao_ref):
    tiles_per_seq = SEQ // ROW_TILE
    is_first = pl.program_id(0) % tiles_per_seq == 0

    @pl.when(is_first)
    def _():
        kpad_ref[:, :, 0:BLOCK, :] = jnp.zeros((N_KV_HEADS, 2, BLOCK, V7X_LANES), BF16)
        vt_ref[:, :, :, 0:BLOCK] = jnp.zeros((N_KV_HEADS, 2, V7X_LANES, BLOCK), BF16)

    x = x_ref[...]
    h = _rms_norm(x, gpre_ref[...]).astype(BF16)

    q = _dot(h, wqkv_ref[:, 0:Q_DIM]) + bqkv_ref[:, 0:Q_DIM]
    q_ref[...] = (q * (HEAD_DIM ** -0.5 * LOG2_E)).astype(BF16)

    new_rows = slice(BLOCK, BLOCK + ROW_TILE)
    k = _dot(h, wqkv_ref[:, Q_DIM:Q_DIM + KV_DIM]) + bqkv_ref[:, Q_DIM:Q_DIM + KV_DIM]
    lo = lax.broadcasted_iota(jnp.int32, (ROW_TILE, V7X_LANES), 1) < HALF_LANES
    for j in range(KV_DIM // V7X_LANES):
        slab = k[:, j * V7X_LANES:(j + 1) * V7X_LANES]
        swapped = pltpu.roll(slab, HALF_LANES, axis=1)
        kpad_ref[2 * j, 0, new_rows, :] = jnp.where(lo, slab, 0.0).astype(BF16)
        kpad_ref[2 * j, 1, new_rows, :] = jnp.where(lo, 0.0, swapped).astype(BF16)
        kpad_ref[2 * j + 1, 0, new_rows, :] = jnp.where(lo, swapped, 0.0).astype(BF16)
        kpad_ref[2 * j + 1, 1, new_rows, :] = jnp.where(lo, 0.0, slab).astype(BF16)

    v = _dot(h, wqkv_ref[:, Q_DIM + KV_DIM:QKV_DIM]) + bqkv_ref[:, Q_DIM + KV_DIM:QKV_DIM]
    zeros_half = jnp.zeros((HALF_LANES, ROW_TILE), BF16)
    for j in range(KV_DIM // V7X_LANES):
        slab_t = v[:, j * V7X_LANES:(j + 1) * V7X_LANES].T.astype(BF16)
        for half in range(2):
            head_t = slab_t[half * HALF_LANES:(half + 1) * HALF_LANES]
            vt_ref[2 * j + half, 0, :, new_rows] = jnp.concatenate([head_t, zeros_half], axis=0)
            vt_ref[2 * j + half, 1, :, new_rows] = jnp.concatenate([zeros_half, head_t], axis=0)

    krow = lax.broadcasted_iota(jnp.int32, (2 * KEYS, PAIR_COLS), 0)
    drop_prev = jnp.logical_and((krow % KEYS) < BLOCK, is_first)

    def scores(b, g):
        qrows = slice(b * BLOCK, (b + 1) * BLOCK)
        krows = slice(b * BLOCK, b * BLOCK + KEYS)
        q2 = jnp.concatenate(
            [q_ref[qrows, (2 * g) * V7X_LANES:(2 * g + 1) * V7X_LANES],
             q_ref[qrows, (2 * g + 1) * V7X_LANES:(2 * g + 2) * V7X_LANES]], axis=0)
        kcat = jnp.concatenate([kpad_ref[g, 0, krows, :], kpad_ref[g, 1, krows, :]], axis=0)
        st = lax.dot_general(kcat, q2, (((1,), (1,)), ((), ())),
                             preferred_element_type=F32)
        bias = bias_ref[g]
        if b == 0:
            bias = jnp.where(drop_prev, -jnp.inf, bias)
        return st + bias

    def softmax(g, st):
        probs, inv = [], []
        for e in range(2):
            se = st[e * KEYS:(e + 1) * KEYS]
            sink = sink_ref[2 * g + e:2 * g + e + 1, :]
            m = jnp.maximum(jnp.max(se, axis=0, keepdims=True), sink)
            p = jnp.exp2(se - m)
            denom = jnp.sum(p, axis=0, keepdims=True) + jnp.exp2(sink - m)
            probs.append(p.astype(BF16))
            inv.append(jnp.broadcast_to(1.0 / denom, (HALF_LANES, PAIR_COLS)))
        return jnp.concatenate(probs, axis=0), jnp.concatenate(inv, axis=0)

    def weighted_values(b, g, pt, inv):
        qrows = slice(b * BLOCK, (b + 1) * BLOCK)
        krows = slice(b * BLOCK, b * BLOCK + KEYS)
        vcat_t = jnp.concatenate([vt_ref[g, 0, :, krows], vt_ref[g, 1, :, krows]], axis=1)
        ot = _dot(vcat_t, pt) * inv
        o2 = ot.T.astype(BF16)
        ao_ref[qrows, (2 * g) * V7X_LANES:(2 * g + 1) * V7X_LANES] = o2[0:BLOCK]
        ao_ref[qrows, (2 * g + 1) * V7X_LANES:(2 * g + 2) * V7X_LANES] = o2[BLOCK:]

    items = [(b, g) for b in range(ROW_TILE // BLOCK) for g in range(N_KV_HEADS)]
    st_of, p_of = {}, {}
    for step in range(len(items) + 2):
        if step < len(items):
            st_of[step] = scores(*items[step])
        if 0 <= step - 1 < len(items):
            p_of[step - 1] = softmax(items[step - 1][1], st_of.pop(step - 1))
        if 0 <= step - 2 < len(items):
            weighted_values(*items[step - 2], *p_of.pop(step - 2))

    kpad_ref[:, :, 0:BLOCK, :] = kpad_ref[:, :, ROW_TILE:ROW_TILE + BLOCK, :]
    vt_ref[:, :, :, 0:BLOCK] = vt_ref[:, :, :, ROW_TILE:ROW_TILE + BLOCK]

    y = _dot(ao_ref[...], wo_ref[...]) + bo_ref[...]
    o_ref[...] = x + _rms_norm(y, gpost_ref[...])


def _attn_tables(sinks):
    hh = jnp.arange(1, N_HEADS + 1, dtype=F32)
    slopes = jnp.exp2(-ALIBI_MAX_BIAS * hh / N_HEADS)
    qi = jnp.arange(BLOCK)[:, None]
    kj = jnp.arange(KEYS)[None, :]
    dist = qi + BLOCK - kj
    valid = (dist >= 0) & (dist < WINDOW)
    bias_h = jnp.where(valid[None], -(slopes[:, None, None] * dist.astype(F32)[None]),
                       -jnp.inf)
    bias = bias_h.reshape(N_KV_HEADS, 2, 2, BLOCK, KEYS).transpose(0, 2, 4, 1, 3)
    bias = bias.reshape(N_KV_HEADS, 2 * KEYS, PAIR_COLS) * LOG2_E
    sink = (sinks.astype(F32) * LOG2_E).reshape(N_KV_HEADS, 2, 2).transpose(0, 2, 1)
    sink = jnp.broadcast_to(sink[..., None], (N_KV_HEADS, 2, 2, BLOCK))
    return bias, sink.reshape(2 * N_KV_HEADS, PAIR_COLS)


def _attn_layer(x2d, layer, g_pre, g_post, w_qkv_stack, b_qkv, sinks, w_o_stack, b_o):
    n_rows = x2d.shape[0]
    bias, sink = _attn_tables(sinks)
    return pl.pallas_call(
        _attn_kernel,
        out_shape=jax.ShapeDtypeStruct((n_rows, D_MODEL), F32),
        grid=(n_rows // ROW_TILE,),
        in_specs=[
            _row_tiled(D_MODEL),
            _resident((1, D_MODEL)),
            _resident((1, D_MODEL)),
            _resident_layer((D_MODEL, QKV_DIM), layer),
            _resident((1, QKV_DIM)),
            _resident((N_KV_HEADS, 2 * KEYS, PAIR_COLS)),
            _resident((2 * N_KV_HEADS, PAIR_COLS)),
            _resident_layer((Q_DIM, D_MODEL), layer),
            _resident((1, D_MODEL)),
        ],
        out_specs=_row_tiled(D_MODEL),
        scratch_shapes=[
            pltpu.VMEM((ROW_TILE, Q_DIM), BF16),
            pltpu.VMEM((N_KV_HEADS, 2, BLOCK + ROW_TILE, V7X_LANES), BF16),
            pltpu.VMEM((N_KV_HEADS, 2, V7X_LANES, BLOCK + ROW_TILE), BF16),
            pltpu.VMEM((ROW_TILE, Q_DIM), BF16),
        ],
        compiler_params=_compiler_params(),
        name="swa_layer",
    )(x2d, g_pre.reshape(1, D_MODEL), g_post.reshape(1, D_MODEL),
      w_qkv_stack, b_qkv.reshape(1, QKV_DIM), bias, sink,
      w_o_stack, b_o.reshape(1, D_MODEL))


def kernel(x, attn_w_qkv, attn_b_qkv, attn_sinks, attn_w_o, attn_b_o, sgu_w_in, sgu_ln_g, sgu_ln_b, sgu_w_s, sgu_b_s, sgu_w_out, ffn_w_in, ffn_conv_w, ffn_conv_b, ffn_w_out, norm_mix_pre, norm_mix_post, norm_ffn_pre, norm_ffn_post):
    batch, seq, d_model = x.shape
    assert (seq, d_model) == (SEQ, D_MODEL) and SEQ % ROW_TILE == 0
    depth = ffn_w_in.shape[0]
    x2d = x.reshape(batch * seq, d_model)
    attn_w_qkv, attn_w_o, sgu_w_in, sgu_w_out, ffn_w_in, ffn_w_out = (
        w.astype(BF16) for w in (attn_w_qkv, attn_w_o, sgu_w_in, sgu_w_out, ffn_w_in, ffn_w_out))
    for i in range(depth):
        j = i // 2
        if i % 2 == 0:
            x2d = _attn_layer(x2d, j, norm_mix_pre[i], norm_mix_post[i], attn_w_qkv,
                              attn_b_qkv[j], attn_sinks[j], attn_w_o, attn_b_o[j])
        else:
            x2d = _sgu_layer(x2d, j, norm_mix_pre[i], norm_mix_post[i], sgu_w_in,
                             sgu_ln_g[j], sgu_ln_b[j], sgu_w_s[j], sgu_b_s[j], sgu_w_out)
        x2d = _ffn_layer(x2d, i, norm_ffn_pre[i], norm_ffn_post[i], ffn_w_in,
                         ffn_conv_w[i], ffn_conv_b[i], ffn_w_out)
    return x2d.reshape(batch, seq, d_model)
```
